```python
import math
import jax, jax.numpy as jnp
from jax import lax
import numpy as np

D_MODEL = 1024
BATCH = 8
SEQ = 2048
DEPTH = 4

MIX_WIDTH = D_MODEL
POOL_WIDTH = MIX_WIDTH // 2
POOL_WINDOWS = (2, 4, 8, 16)
POOL_GROUPS = len(POOL_WINDOWS)
POOL_GC = POOL_WIDTH // POOL_GROUPS
N_HEADS = 4
QK_NOPE = 128
QK_ROPE = 64
V_HEAD = 128
QK_HEAD = QK_NOPE + QK_ROPE
MLA_WIDTH = N_HEADS * V_HEAD
Q_LORA = 384
KV_LORA = 256
ROPE_THETA = 10000.0
SOFTMAX_SCALE = 1.0 / math.sqrt(QK_HEAD)
Q_BLOCK = 128
IN_COLS = POOL_WIDTH + Q_LORA + KV_LORA + QK_ROPE
D_FF = 2816
N_SUBLAYERS = 3
EPS = 1e-6

kernel_name = "hybrid_macaron_pool_mla_adaln"


def rms_norm(x, g):
    xf = x.astype(jnp.float32)
    y = xf * lax.rsqrt(jnp.mean(xf * xf, axis=-1, keepdims=True) + EPS)
    return (y * g.astype(jnp.float32)).astype(x.dtype)


def modulate(h, shift, scale):
    return h * (1 + scale[:, None, :]) + shift[:, None, :]


def swiglu(h, w_gate, w_up, w_down):
    return (jax.nn.silu(h @ w_gate) * (h @ w_up)) @ w_down


def rotate_half(x):
    x1, x2 = jnp.split(x, 2, axis=-1)
    return jnp.concatenate([-x2, x1], axis=-1)


def apply_rope(x, cos, sin):
    return x * cos + rotate_half(x) * sin


def causal_multiscale_pool(u, pool_w, pool_scale):
    B, S, C = u.shape
    cs = jnp.cumsum(u.astype(jnp.float32), axis=1)
    pos = jnp.arange(S)
    means = []
    for g, w in enumerate(POOL_WINDOWS):
        csg = cs[..., g * POOL_GC:(g + 1) * POOL_GC]
        lag = jnp.pad(csg, ((0, 0), (w, 0), (0, 0)))[:, :S]
        cnt = jnp.minimum(pos + 1, w).astype(jnp.float32)[None, :, None]
        means.append((csg - lag) / cnt)
    pooled = jnp.stack(means, axis=2).astype(u.dtype)
    diff = pooled - u.reshape(B, S, POOL_GROUPS, POOL_GC)
    y = jnp.einsum('bsgc,gcd->bsgd', diff, pool_w).reshape(B, S, C)
    return y * pool_scale


def mla_attention(cq, ckv, kr, q_a_norm, w_q_b, kv_a_norm, w_kv_b, cos, sin):
    B, S, _ = cq.shape
    q = (rms_norm(cq, q_a_norm) @ w_q_b).reshape(B, S, N_HEADS, QK_HEAD)
    q_nope, q_rope = q[..., :QK_NOPE], q[..., QK_NOPE:]
    q_rope = apply_rope(q_rope, cos[:, :, None, :], sin[:, :, None, :])
    kv = (rms_norm(ckv, kv_a_norm) @ w_kv_b).reshape(B, S, N_HEADS, QK_NOPE + V_HEAD)
    k_nope, v = kv[..., :QK_NOPE], kv[..., QK_NOPE:]
    k_rope = apply_rope(kr, cos, sin)

    nb = S // Q_BLOCK
    qn = q_nope.reshape(B, nb, Q_BLOCK, N_HEADS, QK_NOPE).transpose(1, 0, 3, 2, 4)
    qr = q_rope.reshape(B, nb, Q_BLOCK, N_HEADS, QK_ROPE).transpose(1, 0, 3, 2, 4)
    kn = k_nope.transpose(0, 2, 1, 3)
    vv = v.transpose(0, 2, 1, 3)
    kpos = jnp.arange(S)

    def block(args):
        qn_b, qr_b, i = args
        s = (jnp.einsum('bhqd,bhkd->bhqk', qn_b, kn)
             + jnp.einsum('bhqd,bkd->bhqk', qr_b, k_rope)).astype(jnp.float32) * SOFTMAX_SCALE
        qpos = i * Q_BLOCK + jnp.arange(Q_BLOCK)
        s = jnp.where(qpos[:, None] >= kpos[None, :], s, -jnp.inf)
        p = jax.nn.softmax(s, axis=-1).astype(vv.dtype)
        return jnp.einsum('bhqk,bhkd->bhqd', p, vv)

    o = lax.map(block, (qn, qr, jnp.arange(nb)))
    return o.transpose(1, 0, 3, 2, 4).reshape(B, S, MLA_WIDTH)


def setup_inputs(seed: int = 0) -> dict:
    key = jax.random.key(seed)
    ks = jax.random.split(key, 24)
    f32 = jnp.float32
    nrm = lambda k, shape, s: (jax.random.normal(k, shape, f32) * s)
    gain = lambda k, shape: 1.0 + 0.05 * jax.random.normal(k, shape, f32)
    D, F, L = D_MODEL, D_FF, DEPTH
    x = jax.random.normal(ks[0], (BATCH, SEQ, D), f32)
    c = jax.random.normal(ks[1], (BATCH, D), f32)
    positions = jnp.broadcast_to(jnp.arange(SEQ, dtype=jnp.int32)[None, :], (BATCH, SEQ))
    return {
        "x": x,
        "c": c,
        "positions": positions,
        "ada_w": nrm(ks[2], (L, D, 3 * N_SUBLAYERS * D), 0.5 * D ** -0.5),
        "ada_b": nrm(ks[3], (L, 3 * N_SUBLAYERS * D), 0.01),
        "ffn1_norm": gain(ks[4], (L, D)),
        "ffn1_w_gate": nrm(ks[5], (L, D, F), D ** -0.5),
        "ffn1_w_up": nrm(ks[6], (L, D, F), D ** -0.5),
        "ffn1_w_down": nrm(ks[7], (L, F, D), F ** -0.5),
        "mix_norm": gain(ks[8], (L, D)),
        "w_in": nrm(ks[9], (L, D, IN_COLS), D ** -0.5),
        "pool_w": nrm(ks[10], (L, POOL_GROUPS, POOL_GC, POOL_GC), POOL_GC ** -0.5),
        "pool_scale": gain(ks[11], (L, POOL_WIDTH)),
        "q_a_norm": gain(ks[12], (L, Q_LORA)),
        "w_q_b": nrm(ks[13], (L, Q_LORA, N_HEADS * QK_HEAD), Q_LORA ** -0.5),
        "kv_a_norm": gain(ks[14], (L, KV_LORA)),
        "w_kv_b": nrm(ks[15], (L, KV_LORA, N_HEADS * (QK_NOPE + V_HEAD)), KV_LORA ** -0.5),
        "w_out": nrm(ks[16], (L, MIX_WIDTH, D), MIX_WIDTH ** -0.5),
        "ffn2_norm": gain(ks[17], (L, D)),
        "ffn2_w_gate": nrm(ks[18], (L, D, F), D ** -0.5),
        "ffn2_w_up": nrm(ks[19], (L, D, F), D ** -0.5),
        "ffn2_w_down": nrm(ks[20], (L, F, D), F ** -0.5),
        "final_norm": gain(ks[21], (D,)),
    }


def reference(x, c, positions, ada_w, ada_b, ffn1_norm, ffn1_w_gate, ffn1_w_up, ffn1_w_down,
              mix_norm, w_in, pool_w, pool_scale, q_a_norm, w_q_b, kv_a_norm, w_kv_b, w_out,
              ffn2_norm, ffn2_w_gate, ffn2_w_up, ffn2_w_down, final_norm):
    inv_freq = 1.0 / (ROPE_THETA ** (jnp.arange(0, QK_ROPE, 2, dtype=jnp.float32) / QK_ROPE))
    ang = positions.astype(jnp.float32)[..., None] * inv_freq
    ang = jnp.concatenate([ang, ang], axis=-1)
    cos = jnp.cos(ang).astype(x.dtype)
    sin = jnp.sin(ang).astype(x.dtype)
    c_act = jax.nn.silu(c)

    for l in range(DEPTH):
        mod = c_act @ ada_w[l] + ada_b[l]
        (sh1, sc1, g1, sh2, sc2, g2, sh3, sc3, g3) = jnp.split(mod, 3 * N_SUBLAYERS, axis=-1)

        h = modulate(rms_norm(x, ffn1_norm[l]), sh1, sc1)
        x = x + 0.5 * g1[:, None, :] * swiglu(h, ffn1_w_gate[l], ffn1_w_up[l], ffn1_w_down[l])

        h = modulate(rms_norm(x, mix_norm[l]), sh2, sc2)
        z = h @ w_in[l]
        o1 = POOL_WIDTH
        o2 = o1 + Q_LORA
        o3 = o2 + KV_LORA
        y_pool = causal_multiscale_pool(z[..., :o1], pool_w[l], pool_scale[l])
        y_mla = mla_attention(z[..., o1:o2], z[..., o2:o3], z[..., o3:], q_a_norm[l], w_q_b[l],
                              kv_a_norm[l], w_kv_b[l], cos, sin)
        y = jnp.concatenate([y_pool, y_mla], axis=-1) @ w_out[l]
        x = x + g2[:, None, :] * y

        h = modulate(rms_norm(x, ffn2_norm[l]), sh3, sc3)
        x = x + 0.5 * g3[:, None, :] * swiglu(h, ffn2_w_gate[l], ffn2_w_up[l], ffn2_w_down[l])

    return rms_norm(x, final_norm)
```

```python
import functools
import math

import jax
import jax.numpy as jnp
from jax import lax
from jax.experimental import pallas as pl
from jax.experimental.pallas import tpu as pltpu

F32 = jnp.float32
BF16 = jnp.bfloat16

D_MODEL = 1024
D_FF = 2816
N_SUB = 9
POOL_WIDTH = 512
POOL_WINDOWS = (2, 4, 8, 16)
POOL_GC = 128
POOL_HALO = 16
N_HEADS = 4
QK_NOPE = 128
QK_ROPE = 64
ROPE_HALF = QK_ROPE // 2
V_HEAD = 128
QK_HEAD = QK_NOPE + QK_ROPE
Q_LORA = 384
KV_LORA = 256
ROPE_THETA = 10000.0
SOFTMAX_SCALE = 1.0 / math.sqrt(QK_HEAD)
EPS = 1e-6
LANES = 128

TOKEN_TILE = 512
ATTN_TILE = 512
VMEM_LIMIT = 56 * 1024 * 1024


def _rms(x, g):
    return x * lax.rsqrt(jnp.mean(x * x, axis=-1, keepdims=True) + EPS) * g


def _dot(a, b):
    return jnp.dot(a, b, preferred_element_type=F32)


def _ada_kernel(c_ref, w_ref, b_ref, o_ref):
    c = c_ref[...]
    ca = (c * jax.nn.sigmoid(c)).astype(BF16)
    o_ref[...] = _dot(ca, w_ref[...].astype(BF16)) + b_ref[...]


def _ada_mod(c, ada_w, ada_b):
    n_layers = ada_w.shape[0]
    batch = c.shape[0]
    out = pl.pallas_call(
        _ada_kernel,
        grid=(n_layers, N_SUB),
        in_specs=[
            pl.BlockSpec((batch, D_MODEL), lambda l, j: (0, 0)),
            pl.BlockSpec((None, D_MODEL, D_MODEL), lambda l, j: (l, 0, j)),
            pl.BlockSpec((None, None, 1, D_MODEL), lambda l, j: (l, j, 0, 0)),
        ],
        out_specs=pl.BlockSpec((None, None, batch, D_MODEL), lambda l, j: (l, j, 0, 0)),
        out_shape=jax.ShapeDtypeStruct((n_layers, N_SUB, batch, D_MODEL), F32),
        compiler_params=pltpu.CompilerParams(dimension_semantics=("arbitrary", "arbitrary")),
        name="ada_mod",
    )(c, ada_w, ada_b.reshape(n_layers, N_SUB, 1, D_MODEL))
    return out.reshape(n_layers, N_SUB, batch, 1, D_MODEL)


def _mod_spec(layer, chunk, tiles_per_batch):
    return pl.BlockSpec((None, None, None, 1, D_MODEL),
                        lambda i: (layer, chunk, i // tiles_per_batch, 0, 0))


def _resident(shape, index_map):
    return pl.BlockSpec(shape, index_map, pipeline_mode=pl.Buffered(1))


def _ffn_kernel(x_ref, g_ref, sh_ref, sc_ref, gt_ref, wg_ref, wu_ref, wd_ref, fin_ref, o_ref,
                *, final_norm):
    x = x_ref[...]
    h = _rms(x, g_ref[...]) * (1.0 + sc_ref[...]) + sh_ref[...]
    hb = h.astype(BF16)
    gate = _dot(hb, wg_ref[...])
    up = _dot(hb, wu_ref[...])
    act = (gate * jax.nn.sigmoid(gate) * up).astype(BF16)
    y = x + (0.5 * gt_ref[...]) * _dot(act, wd_ref[...])
    if final_norm:
        y = _rms(y, fin_ref[...])
    o_ref[...] = y


def _ffn(x, mod, norm, wg, wu, wd, fin, *, layer, chunk0, seq, final_norm):
    tokens = x.shape[0]
    tm = TOKEN_TILE
    tpb = seq // tm
    row = pl.BlockSpec((tm, D_MODEL), lambda i: (i, 0))
    return pl.pallas_call(
        functools.partial(_ffn_kernel, final_norm=final_norm),
        grid=(tokens // tm,),
        in_specs=[
            row,
            pl.BlockSpec((None, 1, D_MODEL), lambda i: (layer, 0, 0)),
            _mod_spec(layer, chunk0, tpb),
            _mod_spec(layer, chunk0 + 1, tpb),
            _mod_spec(layer, chunk0 + 2, tpb),
            _resident((None, D_MODEL, D_FF), lambda i: (layer, 0, 0)),
            _resident((None, D_MODEL, D_FF), lambda i: (layer, 0, 0)),
            _resident((None, D_FF, D_MODEL), lambda i: (layer, 0, 0)),
            pl.BlockSpec((1, D_MODEL), lambda i: (0, 0)),
        ],
        out_specs=row,
        out_shape=jax.ShapeDtypeStruct((tokens, D_MODEL), F32),
        compiler_params=pltpu.CompilerParams(dimension_semantics=("arbitrary",),
                                             vmem_limit_bytes=VMEM_LIMIT),
        name="ffn",
    )(x, norm, mod, mod, mod, wg, wu, wd, fin)


def _mix_in_kernel(x_ref, g_ref, sh_ref, sc_ref, win_ref, pw_ref, ps_ref, qg_ref, wq_ref,
                   kg_ref, wkv_ref, cos_ref, sin_ref,
                   yp_ref, qn_ref, qr_ref, kn_ref, kr_ref, v_ref, ext_ref, *, tiles_per_batch):
    tm = x_ref.shape[0]
    tile_in_batch = pl.program_id(0) % tiles_per_batch
    x = x_ref[...]
    h = _rms(x, g_ref[...]) * (1.0 + sc_ref[...]) + sh_ref[...]
    z = _dot(h.astype(BF16), win_ref[...])
    o_q = POOL_WIDTH
    o_kv = o_q + Q_LORA
    o_kr = o_kv + KV_LORA
    cos = cos_ref[...]
    sin = sin_ref[...]

    u = z[:, :POOL_WIDTH]

    @pl.when(tile_in_batch == 0)
    def _():
        ext_ref[:POOL_HALO, :] = jnp.zeros((POOL_HALO, POOL_WIDTH), F32)

    ext_ref[POOL_HALO:, :] = u
    pos = tile_in_batch * tm + lax.broadcasted_iota(jnp.int32, (tm, 1), 0)
    parts = []
    for g, w in enumerate(POOL_WINDOWS):
        cols = slice(g * POOL_GC, (g + 1) * POOL_GC)
        acc = ext_ref[:, cols]
        span = 1
        while span < w:
            acc = acc + pltpu.roll(acc, span, axis=0)
            span *= 2
        cnt = jnp.minimum(pos + 1, w).astype(F32)
        ug = u[:, cols]
        diff = acc[POOL_HALO:, :] / cnt - ug
        parts.append(_dot(diff.astype(BF16), pw_ref[g]))
    yp_ref[...] = (jnp.concatenate(parts, axis=1) * ps_ref[...]).astype(BF16)
    ext_ref[:POOL_HALO, :] = u[tm - POOL_HALO:, :]

    cq = _rms(z[:, o_q:o_kv], qg_ref[...])
    q = _dot(cq.astype(BF16), wq_ref[...])
    n_nope = N_HEADS * QK_NOPE
    qn_ref[...] = q[:, :n_nope].astype(BF16)
    for p in range(2):
        a = q[:, n_nope + p * LANES:n_nope + (p + 1) * LANES]
        b = q[:, n_nope + (2 + p) * LANES:n_nope + (3 + p) * LANES]
        qr_ref[:, p * LANES:(p + 1) * LANES] = (a * cos + b * sin).astype(BF16)

    ckv = _rms(z[:, o_kv:o_kr], kg_ref[...])
    kv = _dot(ckv.astype(BF16), wkv_ref[...])
    kn_ref[...] = kv[:, :n_nope].astype(BF16)
    v_ref[...] = kv[:, n_nope:].astype(BF16)
    kr_ref[...] = (z[:, o_kr:o_kr + LANES] * cos + z[:, o_kr + LANES:] * sin).astype(BF16)


def _mix_in(x, mod, norm, win, pw, ps, qg, wq, kg, wkv, cos_t, sin_t, *, layer, seq):
    tokens = x.shape[0]
    tm = TOKEN_TILE
    tpb = seq // tm
    lsel3 = lambda i: (layer, 0, 0)

    def row(width):
        return pl.BlockSpec((tm, width), lambda i: (i, 0))

    def out(width):
        return jax.ShapeDtypeStruct((tokens, width), BF16)

    n_nope = N_HEADS * QK_NOPE
    return pl.pallas_call(
        functools.partial(_mix_in_kernel, tiles_per_batch=tpb),
        grid=(tokens // tm,),
        in_specs=[
            row(D_MODEL),
            pl.BlockSpec((None, 1, D_MODEL), lsel3),
            _mod_spec(layer, 3, tpb),
            _mod_spec(layer, 4, tpb),
            pl.BlockSpec((None,) + win.shape[1:], lsel3),
            pl.BlockSpec((None,) + pw.shape[1:], lambda i: (layer, 0, 0, 0)),
            pl.BlockSpec((None, 1, POOL_WIDTH), lsel3),
            pl.BlockSpec((None, 1, Q_LORA), lsel3),
            pl.BlockSpec((None,) + wq.shape[1:], lsel3),
            pl.BlockSpec((None, 1, KV_LORA), lsel3),
            pl.BlockSpec((None,) + wkv.shape[1:], lsel3),
            row(LANES),
            row(LANES),
        ],
        out_specs=[row(POOL_WIDTH), row(n_nope), row(2 * LANES), row(n_nope), row(LANES),
                   row(N_HEADS * V_HEAD)],
        out_shape=[out(POOL_WIDTH), out(n_nope), out(2 * LANES), out(n_nope), out(LANES),
                   out(N_HEADS * V_HEAD)],
        scratch_shapes=[pltpu.VMEM((POOL_HALO + tm, POOL_WIDTH), F32)],
        compiler_params=pltpu.CompilerParams(dimension_semantics=("arbitrary",),
                                             vmem_limit_bytes=VMEM_LIMIT),
        name="mix_in",
    )(x, norm, mod, mod, win, pw, ps, qg, wq, kg, wkv, cos_t, sin_t)


def _attn_kernel(x_ref, gt_ref, yp_ref, qn_ref, qr_ref, kn_ref, kr_ref, v_ref, wo_ref, o_ref,
                 m_ref, l_ref, acc_ref):
    tq = x_ref.shape[0]
    tk = tq
    qi = pl.program_id(1)
    lane = lax.broadcasted_iota(jnp.int32, (1, LANES), 1)
    causal = (lax.broadcasted_iota(jnp.int32, (tq, tk), 0)
              >= lax.broadcasted_iota(jnp.int32, (tq, tk), 1))
    heads = []
    for h in range(N_HEADS):
        pair = qr_ref[:, (h // 2) * LANES:(h // 2 + 1) * LANES]
        mine = (lane >= (h % 2) * QK_ROPE) & (lane < (h % 2 + 1) * QK_ROPE)
        q_h = jnp.concatenate([qn_ref[:, h * QK_NOPE:(h + 1) * QK_NOPE],
                               jnp.where(mine, pair, jnp.zeros_like(pair))], axis=1)
        m_ref[...] = jnp.full(m_ref.shape, -jnp.inf, F32)
        l_ref[...] = jnp.zeros(l_ref.shape, F32)
        acc_ref[...] = jnp.zeros(acc_ref.shape, F32)

        def step(j, masked, q_h=q_h, h=h):
            rows = pl.ds(pl.multiple_of(j * tk, tk), tk)
            k_h = jnp.concatenate([kn_ref[rows, h * QK_NOPE:(h + 1) * QK_NOPE], kr_ref[rows, :]],
                                  axis=1)
            s = lax.dot_general(q_h, k_h, (((1,), (1,)), ((), ())),
                                preferred_element_type=F32) * SOFTMAX_SCALE
            if masked:
                s = jnp.where(causal, s, -jnp.inf)
            m_old = m_ref[...]
            m_new = jnp.maximum(m_old, jnp.max(s, axis=1, keepdims=True))
            alpha = jnp.exp(m_old - m_new)
            p = jnp.exp(s - m_new)
            l_ref[...] = alpha * l_ref[...] + jnp.sum(p, axis=1, keepdims=True)
            acc_ref[...] = alpha * acc_ref[...] + _dot(p.astype(BF16),
                                                       v_ref[rows, h * V_HEAD:(h + 1) * V_HEAD])
            m_ref[...] = m_new

        def body(j, carry):
            step(j, False)
            return carry

        lax.fori_loop(0, qi, body, 0)
        step(qi, True)
        heads.append((acc_ref[...] / l_ref[...]).astype(BF16))
    y_in = jnp.concatenate([yp_ref[...]] + heads, axis=1)
    o_ref[...] = x_ref[...] + gt_ref[...] * _dot(y_in, wo_ref[...])


def _attn(x, mod, yp, qn, qr, kn, kr, v, wo, *, layer, batch, seq):
    tokens = x.shape[0]
    tq = ATTN_TILE
    nq = seq // tq

    def qrow(width):
        return pl.BlockSpec((tq, width), lambda b, i: (b * nq + i, 0))

    def kseq(width):
        return pl.BlockSpec((seq, width), lambda b, i: (b, 0))

    return pl.pallas_call(
        _attn_kernel,
        grid=(batch, nq),
        in_specs=[
            qrow(D_MODEL),
            pl.BlockSpec((None, None, None, 1, D_MODEL), lambda b, i: (layer, 5, b, 0, 0)),
            qrow(POOL_WIDTH),
            qrow(N_HEADS * QK_NOPE),
            qrow(2 * LANES),
            kseq(N_HEADS * QK_NOPE),
            kseq(LANES),
            kseq(N_HEADS * V_HEAD),
            pl.BlockSpec((None, D_MODEL, D_MODEL), lambda b, i: (layer, 0, 0)),
        ],
        out_specs=qrow(D_MODEL),
        out_shape=jax.ShapeDtypeStruct((tokens, D_MODEL), F32),
        scratch_shapes=[pltpu.VMEM((tq, 1), F32), pltpu.VMEM((tq, 1), F32),
                        pltpu.VMEM((tq, V_HEAD), F32)],
        compiler_params=pltpu.CompilerParams(dimension_semantics=("arbitrary", "arbitrary"),
                                             vmem_limit_bytes=VMEM_LIMIT),
        name="attn_out",
    )(x, mod, yp, qn, qr, kn, kr, v, wo)


def _pair_cols(w, base, swap):
    cols = []
    for b0 in base:
        first = w[..., b0:b0 + ROPE_HALF]
        second = w[..., b0 + ROPE_HALF:b0 + QK_ROPE]
        cols += [second, first] if swap else [first, second]
    return cols


def _prep_w_in(w_in):
    o_kr = POOL_WIDTH + Q_LORA + KV_LORA
    plain = _pair_cols(w_in, [o_kr, o_kr], False)
    swapped = _pair_cols(w_in, [o_kr, o_kr], True)
    return jnp.concatenate([w_in[..., :o_kr]] + plain + swapped, axis=-1).astype(BF16)


def _prep_w_q(w_q_b):
    nope = [w_q_b[..., h * QK_HEAD:h * QK_HEAD + QK_NOPE] for h in range(N_HEADS)]
    bases = [h * QK_HEAD + QK_NOPE for h in range(N_HEADS)]
    return jnp.concatenate(nope + _pair_cols(w_q_b, bases, False) + _pair_cols(w_q_b, bases, True),
                           axis=-1).astype(BF16)


def _prep_w_kv(w_kv_b):
    width = QK_NOPE + V_HEAD
    k = [w_kv_b[..., h * width:h * width + QK_NOPE] for h in range(N_HEADS)]
    v = [w_kv_b[..., h * width + QK_NOPE:(h + 1) * width] for h in range(N_HEADS)]
    return jnp.concatenate(k + v, axis=-1).astype(BF16)


def _rope_tables(positions):
    inv_freq = 1.0 / (ROPE_THETA ** (jnp.arange(0, QK_ROPE, 2, dtype=F32) / QK_ROPE))
    ang = positions.astype(F32).reshape(-1, 1) * inv_freq
    cos = jnp.cos(ang)
    sin = jnp.sin(ang)
    return jnp.tile(cos, (1, 4)), jnp.concatenate([-sin, sin, -sin, sin], axis=-1)


def kernel(x, c, positions, ada_w, ada_b, ffn1_norm, ffn1_w_gate, ffn1_w_up, ffn1_w_down, mix_norm, w_in, pool_w, pool_scale, q_a_norm, w_q_b, kv_a_norm, w_kv_b, w_out, ffn2_norm, ffn2_w_gate, ffn2_w_up, ffn2_w_down, final_norm):
    batch, seq, d = x.shape
    n_layers = ada_w.shape[0]
    assert d == D_MODEL and seq % TOKEN_TILE == 0 and seq % ATTN_TILE == 0

    mod = _ada_mod(c, ada_w, ada_b)
    cos_t, sin_t = _rope_tables(positions)
    vec = lambda a: a.reshape(n_layers, 1, a.shape[-1])
    bf = lambda a: a.astype(BF16)
    f1 = (bf(ffn1_w_gate), bf(ffn1_w_up), bf(ffn1_w_down))
    f2 = (bf(ffn2_w_gate), bf(ffn2_w_up), bf(ffn2_w_down))
    win, wq, wkv, pw, wo = _prep_w_in(w_in), _prep_w_q(w_q_b), _prep_w_kv(w_kv_b), bf(pool_w), bf(w_out)
    n1, nm, n2 = vec(ffn1_norm), vec(mix_norm), vec(ffn2_norm)
    ps, qg, kg = vec(pool_scale), vec(q_a_norm), vec(kv_a_norm)
    fin = final_norm.reshape(1, D_MODEL)

    xt = x.reshape(batch * seq, D_MODEL)
    for l in range(n_layers):
        xt = _ffn(xt, mod, n1, *f1, fin, layer=l, chunk0=0, seq=seq, final_norm=False)
        yp, qn, qr, kn, kr, v = _mix_in(xt, mod, nm, win, pw, ps, qg, wq, kg, wkv, cos_t, sin_t,
                                        layer=l, seq=seq)
        xt = _attn(xt, mod, yp, qn, qr, kn, kr, v, wo, layer=l, batch=batch, seq=seq)
        xt = _ffn(xt, mod, n2, *f2, fin, layer=l, chunk0=6, seq=seq,
                  final_norm=(l == n_layers - 1))
    return xt.reshape(batch, seq, D_MODEL)
```

```python
import functools
import math

import jax
import jax.numpy as jnp
from jax import lax
from jax.experimental import pallas as pl
from jax.experimental.pallas import tpu as pltpu

F32 = jnp.float32
BF16 = jnp.bfloat16

D_MODEL = 1024
D_FF = 2816
N_SUB = 9
POOL_WIDTH = 512
POOL_WINDOWS = (2, 4, 8, 16)
POOL_GC = 128
POOL_HALO = 16
N_HEADS = 4
QK_NOPE = 128
QK_ROPE = 64
ROPE_HALF = QK_ROPE // 2
V_HEAD = 128
QK_HEAD = QK_NOPE + QK_ROPE
N_NOPE = N_HEADS * QK_NOPE
N_V = N_HEADS * V_HEAD
Q_LORA = 384
KV_LORA = 256
ROPE_THETA = 10000.0
SOFTMAX_SCALE = 1.0 / math.sqrt(QK_HEAD)
EPS = 1e-6
LANES = 128

TOKEN_TILE = 512
ATTN_TILE = 512
VMEM_LIMIT = 56 * 1024 * 1024

_NT = (((1,), (1,)), ((), ()))


def _rms(x, g):
    return x * lax.rsqrt(jnp.mean(x * x, axis=-1, keepdims=True) + EPS) * g


def _dot(a, b):
    return jnp.dot(a, b, preferred_element_type=F32)


def _dot_nt(a, b):
    return lax.dot_general(a, b, _NT, preferred_element_type=F32)


def _ada_kernel(c_ref, w_ref, b_ref, o_ref):
    c = c_ref[...]
    ca = (c * jax.nn.sigmoid(c)).astype(BF16)
    o_ref[...] = _dot(ca, w_ref[...].astype(BF16)) + b_ref[...]


def _ada_mod(c, ada_w, ada_b):
    n_layers = ada_w.shape[0]
    batch = c.shape[0]
    out = pl.pallas_call(
        _ada_kernel,
        grid=(n_layers, N_SUB),
        in_specs=[
            pl.BlockSpec((batch, D_MODEL), lambda l, j: (0, 0)),
            pl.BlockSpec((None, D_MODEL, D_MODEL), lambda l, j: (l, 0, j)),
            pl.BlockSpec((None, None, 1, D_MODEL), lambda l, j: (l, j, 0, 0)),
        ],
        out_specs=pl.BlockSpec((None, None, batch, D_MODEL), lambda l, j: (l, j, 0, 0)),
        out_shape=jax.ShapeDtypeStruct((n_layers, N_SUB, batch, D_MODEL), F32),
        compiler_params=pltpu.CompilerParams(dimension_semantics=("arbitrary", "arbitrary")),
        name="ada_mod",
    )(c, ada_w, ada_b.reshape(n_layers, N_SUB, 1, D_MODEL))
    return out.reshape(n_layers, N_SUB, batch, 1, D_MODEL)


def _mod_spec(layer, chunk, tiles_per_batch):
    return pl.BlockSpec((None, None, None, 1, D_MODEL),
                        lambda i: (layer, chunk, i // tiles_per_batch, 0, 0))


def _resident(shape, index_map):
    return pl.BlockSpec(shape, index_map, pipeline_mode=pl.Buffered(1))


def _ffn_kernel(x_ref, g_ref, sh_ref, sc_ref, gt_ref, wg_ref, wu_ref, wd_ref, fin_ref, o_ref,
                *, final_norm):
    x = x_ref[...]
    h = _rms(x, g_ref[...]) * (1.0 + sc_ref[...]) + sh_ref[...]
    hb = h.astype(BF16)
    gate = _dot(hb, wg_ref[...])
    up = _dot(hb, wu_ref[...])
    act = (gate * jax.nn.sigmoid(gate) * up).astype(BF16)
    y = x + (0.5 * gt_ref[...]) * _dot(act, wd_ref[...])
    if final_norm:
        y = _rms(y, fin_ref[...])
    o_ref[...] = y


def _ffn(x, mod, norm, wg, wu, wd, fin, *, layer, chunk0, seq, final_norm):
    tokens = x.shape[0]
    tm = TOKEN_TILE
    tpb = seq // tm
    row = pl.BlockSpec((tm, D_MODEL), lambda i: (i, 0))
    return pl.pallas_call(
        functools.partial(_ffn_kernel, final_norm=final_norm),
        grid=(tokens // tm,),
        in_specs=[
            row,
            pl.BlockSpec((None, 1, D_MODEL), lambda i: (layer, 0, 0)),
            _mod_spec(layer, chunk0, tpb),
            _mod_spec(layer, chunk0 + 1, tpb),
            _mod_spec(layer, chunk0 + 2, tpb),
            _resident((None, D_MODEL, D_FF), lambda i: (layer, 0, 0)),
            _resident((None, D_MODEL, D_FF), lambda i: (layer, 0, 0)),
            _resident((None, D_FF, D_MODEL), lambda i: (layer, 0, 0)),
            pl.BlockSpec((1, D_MODEL), lambda i: (0, 0)),
        ],
        out_specs=row,
        out_shape=jax.ShapeDtypeStruct((tokens, D_MODEL), F32),
        compiler_params=pltpu.CompilerParams(dimension_semantics=("arbitrary",),
                                             vmem_limit_bytes=VMEM_LIMIT),
        name="ffn",
    )(x, norm, mod, mod, mod, wg, wu, wd, fin)


def _mix_in_kernel(x_ref, g_ref, sh_ref, sc_ref, win_ref, pw_ref, ps_ref, qg_ref, wqt_ref,
                   kg_ref, wk_ref, wvt_ref, cos_ref, sin_ref, cost_ref, sint_ref,
                   yp_ref, qt_ref, kn_ref, kr_ref, vt_ref, ext_ref, *, tiles_per_batch):
    tm = x_ref.shape[0]
    tile_in_batch = pl.program_id(0) % tiles_per_batch
    x = x_ref[...]
    h = _rms(x, g_ref[...]) * (1.0 + sc_ref[...]) + sh_ref[...]
    z = _dot(h.astype(BF16), win_ref[...])
    o_q = POOL_WIDTH
    o_kv = o_q + Q_LORA
    o_kr = o_kv + KV_LORA

    u = z[:, :POOL_WIDTH]

    @pl.when(tile_in_batch == 0)
    def _():
        ext_ref[:POOL_HALO, :] = jnp.zeros((POOL_HALO, POOL_WIDTH), F32)

    ext_ref[POOL_HALO:, :] = u
    pos = tile_in_batch * tm + lax.broadcasted_iota(jnp.int32, (tm, 1), 0)
    parts = []
    for g, w in enumerate(POOL_WINDOWS):
        cols = slice(g * POOL_GC, (g + 1) * POOL_GC)
        acc = ext_ref[:, cols]
        span = 1
        while span < w:
            acc = acc + pltpu.roll(acc, span, axis=0)
            span *= 2
        cnt = jnp.minimum(pos + 1, w).astype(F32)
        diff = acc[POOL_HALO:, :] / cnt - u[:, cols]
        parts.append(_dot(diff.astype(BF16), pw_ref[g]))
    yp_ref[...] = (jnp.concatenate(parts, axis=1) * ps_ref[...]).astype(BF16)
    ext_ref[:POOL_HALO, :] = u[tm - POOL_HALO:, :]

    cq = _rms(z[:, o_q:o_kv], qg_ref[...]).astype(BF16)
    qt = _dot_nt(wqt_ref[...], cq)
    qt_ref[:N_NOPE, :] = qt[:N_NOPE, :].astype(BF16)
    cos_t = cost_ref[...]
    sin_t = sint_ref[...]
    for p in range(2):
        a = qt[N_NOPE + p * LANES:N_NOPE + (p + 1) * LANES, :]
        b = qt[N_NOPE + (2 + p) * LANES:N_NOPE + (3 + p) * LANES, :]
        qt_ref[N_NOPE + p * LANES:N_NOPE + (p + 1) * LANES, :] = (a * cos_t + b * sin_t).astype(BF16)

    ckv = _rms(z[:, o_kv:o_kr], kg_ref[...]).astype(BF16)
    kn_ref[...] = _dot(ckv, wk_ref[...]).astype(BF16)
    vt_ref[...] = _dot_nt(wvt_ref[...], ckv).astype(BF16)
    kr_ref[...] = (z[:, o_kr:o_kr + LANES] * cos_ref[...]
                   + z[:, o_kr + LANES:] * sin_ref[...]).astype(BF16)


def _mix_in(x, mod, norm, win, pw, ps, qg, wqt, kg, wk, wvt, tables, *, layer, seq):
    tokens = x.shape[0]
    tm = TOKEN_TILE
    tpb = seq // tm
    lsel3 = lambda i: (layer, 0, 0)

    def row(width):
        return pl.BlockSpec((tm, width), lambda i: (i, 0))

    def col(height):
        return pl.BlockSpec((height, tm), lambda i: (0, i))

    def whole(w):
        return pl.BlockSpec((None,) + w.shape[1:], lsel3)

    return pl.pallas_call(
        functools.partial(_mix_in_kernel, tiles_per_batch=tpb),
        grid=(tokens // tm,),
        in_specs=[
            row(D_MODEL),
            pl.BlockSpec((None, 1, D_MODEL), lsel3),
            _mod_spec(layer, 3, tpb),
            _mod_spec(layer, 4, tpb),
            whole(win),
            pl.BlockSpec((None,) + pw.shape[1:], lambda i: (layer, 0, 0, 0)),
            pl.BlockSpec((None, 1, POOL_WIDTH), lsel3),
            pl.BlockSpec((None, 1, Q_LORA), lsel3),
            whole(wqt),
            pl.BlockSpec((None, 1, KV_LORA), lsel3),
            whole(wk),
            whole(wvt),
            row(LANES), row(LANES), col(LANES), col(LANES),
        ],
        out_specs=[row(POOL_WIDTH), col(N_NOPE + 2 * LANES), row(N_NOPE), row(LANES), col(N_V)],
        out_shape=[jax.ShapeDtypeStruct((tokens, POOL_WIDTH), BF16),
                   jax.ShapeDtypeStruct((N_NOPE + 2 * LANES, tokens), BF16),
                   jax.ShapeDtypeStruct((tokens, N_NOPE), BF16),
                   jax.ShapeDtypeStruct((tokens, LANES), BF16),
                   jax.ShapeDtypeStruct((N_V, tokens), BF16)],
        scratch_shapes=[pltpu.VMEM((POOL_HALO + tm, POOL_WIDTH), F32)],
        compiler_params=pltpu.CompilerParams(dimension_semantics=("arbitrary",),
                                             vmem_limit_bytes=VMEM_LIMIT),
        name="mix_in",
    )(x, norm, mod, mod, win, pw, ps, qg, wqt, kg, wk, wvt, *tables)


def _attn_head(qt_ref, kn_ref, kr_ref, vt_ref, h, n, tq):
    sub = lax.broadcasted_iota(jnp.int32, (LANES, 1), 0)
    mine = (sub >= (h % 2) * QK_ROPE) & (sub < (h % 2 + 1) * QK_ROPE)
    pair = qt_ref[N_NOPE + (h // 2) * LANES:N_NOPE + (h // 2 + 1) * LANES, :]
    q_h = jnp.concatenate([qt_ref[h * QK_NOPE:(h + 1) * QK_NOPE, :],
                           jnp.where(mine, pair, jnp.zeros_like(pair))], axis=0)

    def keys(lo, hi):
        return jnp.concatenate([kn_ref[lo:hi, h * QK_NOPE:(h + 1) * QK_NOPE], kr_ref[lo:hi, :]],
                               axis=1)

    n0 = n - tq
    key_pos = lax.broadcasted_iota(jnp.int32, (tq, tq), 0)
    qry_pos = lax.broadcasted_iota(jnp.int32, (tq, tq), 1)
    s_d = jnp.where(key_pos <= qry_pos, _dot(keys(n0, n), q_h) * SOFTMAX_SCALE, -jnp.inf)
    m = jnp.max(s_d, axis=0, keepdims=True)
    if n0:
        s_b = _dot(keys(0, n0), q_h) * SOFTMAX_SCALE
        m = jnp.maximum(m, jnp.max(s_b, axis=0, keepdims=True))
    p_d = jnp.exp(s_d - m)
    l = jnp.sum(p_d, axis=0, keepdims=True)
    o = _dot(vt_ref[h * V_HEAD:(h + 1) * V_HEAD, n0:n], p_d.astype(BF16))
    if n0:
        p_b = jnp.exp(s_b - m)
        l = l + jnp.sum(p_b, axis=0, keepdims=True)
        o = o + _dot(vt_ref[h * V_HEAD:(h + 1) * V_HEAD, :n0], p_b.astype(BF16))
    return o / l


def _attn_kernel(x_ref, gt_ref, yp_ref, qt_ref, kn_ref, kr_ref, vt_ref, wo_ref, o_ref, ot_ref):
    tq = x_ref.shape[0]
    n_tiles = kn_ref.shape[0] // tq
    qi = pl.program_id(1)
    for k in range(n_tiles):
        @pl.when(qi == k)
        def _(k=k):
            for h in range(N_HEADS):
                ot_ref[h * V_HEAD:(h + 1) * V_HEAD, :] = _attn_head(
                    qt_ref, kn_ref, kr_ref, vt_ref, h, (k + 1) * tq, tq)
    y_in = jnp.concatenate([yp_ref[...], ot_ref[...].T.astype(BF16)], axis=1)
    o_ref[...] = x_ref[...] + gt_ref[...] * _dot(y_in, wo_ref[...])


def _attn(x, mod, yp, qt, kn, kr, vt, wo, *, layer, batch, seq):
    tokens = x.shape[0]
    tq = ATTN_TILE
    nq = seq // tq

    def qrow(width):
        return pl.BlockSpec((tq, width), lambda b, i: (b * nq + i, 0))

    return pl.pallas_call(
        _attn_kernel,
        grid=(batch, nq),
        in_specs=[
            qrow(D_MODEL),
            pl.BlockSpec((None, None, None, 1, D_MODEL), lambda b, i: (layer, 5, b, 0, 0)),
            qrow(POOL_WIDTH),
            pl.BlockSpec((qt.shape[0], tq), lambda b, i: (0, b * nq + i)),
            pl.BlockSpec((seq, N_NOPE), lambda b, i: (b, 0)),
            pl.BlockSpec((seq, LANES), lambda b, i: (b, 0)),
            pl.BlockSpec((N_V, seq), lambda b, i: (0, b)),
            pl.BlockSpec((None, D_MODEL, D_MODEL), lambda b, i: (layer, 0, 0)),
        ],
        out_specs=qrow(D_MODEL),
        out_shape=jax.ShapeDtypeStruct((tokens, D_MODEL), F32),
        scratch_shapes=[pltpu.VMEM((N_V, tq), F32)],
        compiler_params=pltpu.CompilerParams(dimension_semantics=("arbitrary", "arbitrary"),
                                             vmem_limit_bytes=VMEM_LIMIT),
        name="attn_out",
    )(x, mod, yp, qt, kn, kr, vt, wo)


def _pair_cols(w, base, swap):
    cols = []
    for b0 in base:
        first = w[..., b0:b0 + ROPE_HALF]
        second = w[..., b0 + ROPE_HALF:b0 + QK_ROPE]
        cols += [second, first] if swap else [first, second]
    return cols


def _prep_w_in(w_in):
    o_kr = POOL_WIDTH + Q_LORA + KV_LORA
    plain = _pair_cols(w_in, [o_kr, o_kr], False)
    swapped = _pair_cols(w_in, [o_kr, o_kr], True)
    return jnp.concatenate([w_in[..., :o_kr]] + plain + swapped, axis=-1).astype(BF16)


def _prep_w_q_t(w_q_b):
    nope = [w_q_b[..., h * QK_HEAD:h * QK_HEAD + QK_NOPE] for h in range(N_HEADS)]
    bases = [h * QK_HEAD + QK_NOPE for h in range(N_HEADS)]
    w = jnp.concatenate(nope + _pair_cols(w_q_b, bases, False) + _pair_cols(w_q_b, bases, True),
                        axis=-1)
    return jnp.swapaxes(w, -1, -2).astype(BF16)


def _prep_w_kv(w_kv_b):
    width = QK_NOPE + V_HEAD
    k = [w_kv_b[..., h * width:h * width + QK_NOPE] for h in range(N_HEADS)]
    v = [w_kv_b[..., h * width + QK_NOPE:(h + 1) * width] for h in range(N_HEADS)]
    wk = jnp.concatenate(k, axis=-1).astype(BF16)
    wvt = jnp.swapaxes(jnp.concatenate(v, axis=-1), -1, -2).astype(BF16)
    return wk, wvt


def _rope_tables(positions):
    inv_freq = 1.0 / (ROPE_THETA ** (jnp.arange(0, QK_ROPE, 2, dtype=F32) / QK_ROPE))
    ang = positions.astype(F32).reshape(-1, 1) * inv_freq
    cos = jnp.cos(ang)
    sin = jnp.sin(ang)
    cos = jnp.tile(cos, (1, 4))
    sin = jnp.concatenate([-sin, sin, -sin, sin], axis=-1)
    return cos, sin, cos.T, sin.T


def kernel(x, c, positions, ada_w, ada_b, ffn1_norm, ffn1_w_gate, ffn1_w_up, ffn1_w_down, mix_norm, w_in, pool_w, pool_scale, q_a_norm, w_q_b, kv_a_norm, w_kv_b, w_out, ffn2_norm, ffn2_w_gate, ffn2_w_up, ffn2_w_down, final_norm):
    batch, seq, d = x.shape
    n_layers = ada_w.shape[0]
    assert d == D_MODEL and seq % TOKEN_TILE == 0 and seq % ATTN_TILE == 0

    mod = _ada_mod(c, ada_w, ada_b)
    tables = _rope_tables(positions)
    vec = lambda a: a.reshape(n_layers, 1, a.shape[-1])
    bf = lambda a: a.astype(BF16)
    f1 = (bf(ffn1_w_gate), bf(ffn1_w_up), bf(ffn1_w_down))
    f2 = (bf(ffn2_w_gate), bf(ffn2_w_up), bf(ffn2_w_down))
    win, wqt, pw, wo = _prep_w_in(w_in), _prep_w_q_t(w_q_b), bf(pool_w), bf(w_out)
    wk, wvt = _prep_w_kv(w_kv_b)
    n1, nm, n2 = vec(ffn1_norm), vec(mix_norm), vec(ffn2_norm)
    ps, qg, kg = vec(pool_scale), vec(q_a_norm), vec(kv_a_norm)
    fin = final_norm.reshape(1, D_MODEL)

    xt = x.reshape(batch * seq, D_MODEL)
    for l in range(n_layers):
        xt = _ffn(xt, mod, n1, *f1, fin, layer=l, chunk0=0, seq=seq, final_norm=False)
        yp, qt, kn, kr, vt = _mix_in(xt, mod, nm, win, pw, ps, qg, wqt, kg, wk, wvt, tables,
                                     layer=l, seq=seq)
        xt = _attn(xt, mod, yp, qt, kn, kr, vt, wo, layer=l, batch=batch, seq=seq)
        xt = _ffn(xt, mod, n2, *f2, fin, layer=l, chunk0=6, seq=seq,
                  final_norm=(l == n_layers - 1))
    return xt.reshape(batch, seq, D_MODEL)
```

```python
import functools
import math

import jax
import jax.numpy as jnp
from jax import lax
from jax.experimental import pallas as pl
from jax.experimental.pallas import tpu as pltpu

F32 = jnp.float32
BF16 = jnp.bfloat16

D_MODEL = 1024
D_FF = 2816
N_SUB = 9
POOL_WIDTH = 512
POOL_WINDOWS = (2, 4, 8, 16)
POOL_GC = 128
POOL_HALO = 16
N_HEADS = 4
QK_NOPE = 128
QK_ROPE = 64
ROPE_HALF = QK_ROPE // 2
V_HEAD = 128
QK_HEAD = QK_NOPE + QK_ROPE
N_NOPE = N_HEADS * QK_NOPE
N_V = N_HEADS * V_HEAD
Q_LORA = 384
KV_LORA = 256
ROPE_THETA = 10000.0
SOFTMAX_SCALE = 1.0 / math.sqrt(QK_HEAD)
EPS = 1e-6
LANES = 128

TOKEN_TILE = 512
ATTN_TILE = 512
VMEM_LIMIT = 56 * 1024 * 1024

_NT = (((1,), (1,)), ((), ()))


def _rms(x, g):
    return x * lax.rsqrt(jnp.mean(x * x, axis=-1, keepdims=True) + EPS) * g


def _dot(a, b):
    return jnp.dot(a, b, preferred_element_type=F32)


def _dot_nt(a, b):
    return lax.dot_general(a, b, _NT, preferred_element_type=F32)


def _ada_kernel(c_ref, w_ref, b_ref, o_ref):
    c = c_ref[...]
    ca = (c * jax.nn.sigmoid(c)).astype(BF16)
    o_ref[...] = _dot(ca, w_ref[...].astype(BF16)) + b_ref[...]


def _ada_mod(c, ada_w, ada_b):
    n_layers = ada_w.shape[0]
    batch = c.shape[0]
    out = pl.pallas_call(
        _ada_kernel,
        grid=(n_layers, N_SUB),
        in_specs=[
            pl.BlockSpec((batch, D_MODEL), lambda l, j: (0, 0)),
            pl.BlockSpec((None, D_MODEL, D_MODEL), lambda l, j: (l, 0, j)),
            pl.BlockSpec((None, None, 1, D_MODEL), lambda l, j: (l, j, 0, 0)),
        ],
        out_specs=pl.BlockSpec((None, None, batch, D_MODEL), lambda l, j: (l, j, 0, 0)),
        out_shape=jax.ShapeDtypeStruct((n_layers, N_SUB, batch, D_MODEL), F32),
        compiler_params=pltpu.CompilerParams(dimension_semantics=("arbitrary", "arbitrary")),
        name="ada_mod",
    )(c, ada_w, ada_b.reshape(n_layers, N_SUB, 1, D_MODEL))
    return out.reshape(n_layers, N_SUB, batch, 1, D_MODEL)


def _mod_spec(layer, chunk, tiles_per_batch):
    return pl.BlockSpec((None, None, None, 1, D_MODEL),
                        lambda i: (layer, chunk, i // tiles_per_batch, 0, 0))


def _resident(shape, index_map):
    return pl.BlockSpec(shape, index_map, pipeline_mode=pl.Buffered(1))


def _modulated_norm(x_ref, g_ref, sh_ref, sc_ref):
    return (_rms(x_ref[...], g_ref[...]) * (1.0 + sc_ref[...]) + sh_ref[...]).astype(BF16)


def _ffn_kernel(*refs, final_norm, cast_next):
    x_ref, g_ref, sh_ref, sc_ref, gt_ref, wg_ref, wu_ref, wd_ref, fin_ref = refs[:9]
    rest = refs[9:]
    if cast_next:
        ng_ref, nu_ref, nd_ref, o_ref, cg_ref, cu_ref, cd_ref = rest
        cg_ref[...] = ng_ref[...].astype(BF16)
        cu_ref[...] = nu_ref[...].astype(BF16)
        cd_ref[...] = nd_ref[...].astype(BF16)
    else:
        (o_ref,) = rest
    hb = _modulated_norm(x_ref, g_ref, sh_ref, sc_ref)
    gate = _dot(hb, wg_ref[...])
    up = _dot(hb, wu_ref[...])
    act = (gate * jax.nn.sigmoid(gate) * up).astype(BF16)
    y = x_ref[...] + (0.5 * gt_ref[...]) * _dot(act, wd_ref[...])
    if final_norm:
        y = _rms(y, fin_ref[...])
    o_ref[...] = y


BF16_ROWS = 16


def _ffn(x, mod, norm, wg, wu, wd, fin, nxt, *, layer, chunk0, seq, final_norm):
    tokens = x.shape[0]
    tm = TOKEN_TILE
    tpb = seq // tm
    steps = tokens // tm
    cast_rows = D_MODEL // steps
    cast_down_rows = 2 * D_FF // steps
    assert cast_rows * steps == D_MODEL and cast_down_rows * steps == 2 * D_FF
    assert cast_rows % BF16_ROWS == 0 and cast_down_rows % BF16_ROWS == 0 and steps % 2 == 0
    row = pl.BlockSpec((tm, D_MODEL), lambda i: (i, 0))
    in_specs = [
        row,
        pl.BlockSpec((None, 1, D_MODEL), lambda i: (layer, 0, 0)),
        _mod_spec(layer, chunk0, tpb), _mod_spec(layer, chunk0 + 1, tpb),
        _mod_spec(layer, chunk0 + 2, tpb),
        _resident((D_MODEL, D_FF), lambda i: (0, 0)),
        _resident((D_MODEL, D_FF), lambda i: (0, 0)),
        _resident((D_FF, D_MODEL), lambda i: (0, 0)),
        pl.BlockSpec((1, D_MODEL), lambda i: (0, 0)),
    ]
    args = [x, norm, mod, mod, mod, wg, wu, wd, fin]
    out_specs = [row]
    out_shape = [jax.ShapeDtypeStruct((tokens, D_MODEL), F32)]
    if nxt is not None:
        ng, nu, nd, nl = nxt
        in_specs += [
            pl.BlockSpec((None, cast_rows, D_FF), lambda i: (nl, i, 0)),
            pl.BlockSpec((None, cast_rows, D_FF), lambda i: (nl, i, 0)),
            pl.BlockSpec((None, cast_down_rows, D_MODEL), lambda i: (nl, i // 2, 0)),
        ]
        args += [ng, nu, nd]
        out_specs += [
            pl.BlockSpec((cast_rows, D_FF), lambda i: (i, 0)),
            pl.BlockSpec((cast_rows, D_FF), lambda i: (i, 0)),
            pl.BlockSpec((cast_down_rows, D_MODEL), lambda i: (i // 2, 0)),
        ]
        out_shape += [jax.ShapeDtypeStruct((D_MODEL, D_FF), BF16),
                      jax.ShapeDtypeStruct((D_MODEL, D_FF), BF16),
                      jax.ShapeDtypeStruct((D_FF, D_MODEL), BF16)]
    outs = pl.pallas_call(
        functools.partial(_ffn_kernel, final_norm=final_norm, cast_next=nxt is not None),
        grid=(steps,),
        in_specs=in_specs,
        out_specs=out_specs,
        out_shape=out_shape,
        compiler_params=pltpu.CompilerParams(dimension_semantics=("arbitrary",),
                                             vmem_limit_bytes=VMEM_LIMIT),
        name="ffn",
    )(*args)
    return outs[0], tuple(outs[1:])


def _mix_in_kernel(x_ref, g_ref, sh_ref, sc_ref, win_ref, pw_ref, ps_ref,
                   qg_ref, wqt_ref, kg_ref, wk_ref, wvt_ref, cos_ref, sin_ref, cost_ref, sint_ref,
                   yp_ref, qt_ref, kn_ref, kr_ref, vt_ref, ext_ref, *, tiles_per_batch):
    tm = x_ref.shape[0]
    tile_in_batch = pl.program_id(0) % tiles_per_batch
    hb = _modulated_norm(x_ref, g_ref, sh_ref, sc_ref)
    z = _dot(hb, win_ref[...])
    o_q = POOL_WIDTH
    o_kv = o_q + Q_LORA
    o_kr = o_kv + KV_LORA

    u = z[:, :POOL_WIDTH]

    @pl.when(tile_in_batch == 0)
    def _():
        ext_ref[:POOL_HALO, :] = jnp.zeros((POOL_HALO, POOL_WIDTH), F32)

    ext_ref[POOL_HALO:, :] = u
    pos = tile_in_batch * tm + lax.broadcasted_iota(jnp.int32, (tm, 1), 0)
    parts = []
    for g, w in enumerate(POOL_WINDOWS):
        cols = slice(g * POOL_GC, (g + 1) * POOL_GC)
        acc = ext_ref[:, cols]
        span = 1
        while span < w:
            acc = acc + pltpu.roll(acc, span, axis=0)
            span *= 2
        cnt = jnp.minimum(pos + 1, w).astype(F32)
        diff = acc[POOL_HALO:, :] / cnt - u[:, cols]
        parts.append(_dot(diff.astype(BF16), pw_ref[g]))
    yp_ref[...] = (jnp.concatenate(parts, axis=1) * ps_ref[...]).astype(BF16)
    ext_ref[:POOL_HALO, :] = u[tm - POOL_HALO:, :]

    cq = _rms(z[:, o_q:o_kv], qg_ref[...]).astype(BF16)
    qt = _dot_nt(wqt_ref[...], cq)
    qt_ref[:N_NOPE, :] = qt[:N_NOPE, :].astype(BF16)
    cos_t = cost_ref[...]
    sin_t = sint_ref[...]
    for p in range(2):
        a = qt[N_NOPE + p * LANES:N_NOPE + (p + 1) * LANES, :]
        b = qt[N_NOPE + (2 + p) * LANES:N_NOPE + (3 + p) * LANES, :]
        qt_ref[N_NOPE + p * LANES:N_NOPE + (p + 1) * LANES, :] = (a * cos_t + b * sin_t).astype(BF16)

    ckv = _rms(z[:, o_kv:o_kr], kg_ref[...]).astype(BF16)
    kn_ref[...] = _dot(ckv, wk_ref[...]).astype(BF16)
    vt_ref[...] = _dot_nt(wvt_ref[...], ckv).astype(BF16)
    kr_ref[...] = (z[:, o_kr:o_kr + LANES] * cos_ref[...]
                   + z[:, o_kr + LANES:] * sin_ref[...]).astype(BF16)


def _mix_in(x, mod, norm, win, pw, ps, qg, wqt, kg, wk, wvt, tables, *, layer, seq):
    tokens = x.shape[0]
    tm = TOKEN_TILE
    tpb = seq // tm
    steps = tokens // tm
    lsel3 = lambda i: (layer, 0, 0)

    def row(width):
        return pl.BlockSpec((tm, width), lambda i: (i, 0))

    def col(height):
        return pl.BlockSpec((height, tm), lambda i: (0, i))

    def whole(w):
        return pl.BlockSpec((None,) + w.shape[1:], lsel3)

    return pl.pallas_call(
        functools.partial(_mix_in_kernel, tiles_per_batch=tpb),
        grid=(steps,),
        in_specs=[
            row(D_MODEL),
            pl.BlockSpec((None, 1, D_MODEL), lsel3),
            _mod_spec(layer, 3, tpb),
            _mod_spec(layer, 4, tpb),
            whole(win),
            pl.BlockSpec((None,) + pw.shape[1:], lambda i: (layer, 0, 0, 0)),
            pl.BlockSpec((None, 1, POOL_WIDTH), lsel3),
            pl.BlockSpec((None, 1, Q_LORA), lsel3),
            whole(wqt),
            pl.BlockSpec((None, 1, KV_LORA), lsel3),
            whole(wk),
            whole(wvt),
            row(LANES), row(LANES), col(LANES), col(LANES),
        ],
        out_specs=[row(POOL_WIDTH), col(N_NOPE + 2 * LANES), row(N_NOPE), row(LANES), col(N_V)],
        out_shape=[jax.ShapeDtypeStruct((tokens, POOL_WIDTH), BF16),
                   jax.ShapeDtypeStruct((N_NOPE + 2 * LANES, tokens), BF16),
                   jax.ShapeDtypeStruct((tokens, N_NOPE), BF16),
                   jax.ShapeDtypeStruct((tokens, LANES), BF16),
                   jax.ShapeDtypeStruct((N_V, tokens), BF16)],
        scratch_shapes=[pltpu.VMEM((POOL_HALO + tm, POOL_WIDTH), F32)],
        compiler_params=pltpu.CompilerParams(dimension_semantics=("arbitrary",),
                                             vmem_limit_bytes=VMEM_LIMIT),
        name="mix_in",
    )(x, norm, mod, mod, win, pw, ps, qg, wqt, kg, wk, wvt, *tables)


def _attn_head(qt_ref, kn_ref, kr_ref, vt_ref, h, n, tq):
    sub = lax.broadcasted_iota(jnp.int32, (LANES, 1), 0)
    mine = (sub >= (h % 2) * QK_ROPE) & (sub < (h % 2 + 1) * QK_ROPE)
    pair = qt_ref[N_NOPE + (h // 2) * LANES:N_NOPE + (h // 2 + 1) * LANES, :]
    q_h = jnp.concatenate([qt_ref[h * QK_NOPE:(h + 1) * QK_NOPE, :],
                           jnp.where(mine, pair, jnp.zeros_like(pair))], axis=0)

    def keys(lo, hi):
        return jnp.concatenate([kn_ref[lo:hi, h * QK_NOPE:(h + 1) * QK_NOPE], kr_ref[lo:hi, :]],
                               axis=1)

    n0 = n - tq
    key_pos = lax.broadcasted_iota(jnp.int32, (tq, tq), 0)
    qry_pos = lax.broadcasted_iota(jnp.int32, (tq, tq), 1)
    c = SOFTMAX_SCALE * math.log2(math.e)
    s_d = jnp.where(key_pos <= qry_pos, _dot(keys(n0, n), q_h), -jnp.inf)
    m = jnp.max(s_d, axis=0, keepdims=True)
    if n0:
        s_b = _dot(keys(0, n0), q_h)
        m = jnp.maximum(m, jnp.max(s_b, axis=0, keepdims=True))
    mc = m * c
    p_d = jnp.exp2(s_d * c - mc)
    l = jnp.sum(p_d, axis=0, keepdims=True)
    o = _dot(vt_ref[h * V_HEAD:(h + 1) * V_HEAD, n0:n], p_d.astype(BF16))
    if n0:
        p_b = jnp.exp2(s_b * c - mc)
        l = l + jnp.sum(p_b, axis=0, keepdims=True)
        o = o + _dot(vt_ref[h * V_HEAD:(h + 1) * V_HEAD, :n0], p_b.astype(BF16))
    return o / l


def _attn_kernel(x_ref, gt_ref, yp_ref, qt_ref, kn_ref, kr_ref, vt_ref, wo_ref, o_ref, ot_ref):
    tq = x_ref.shape[0]
    n_tiles = kn_ref.shape[0] // tq
    qi = pl.program_id(1)
    for k in range(n_tiles):
        @pl.when(qi == k)
        def _(k=k):
            for h in range(N_HEADS):
                ot_ref[h * V_HEAD:(h + 1) * V_HEAD, :] = _attn_head(
                    qt_ref, kn_ref, kr_ref, vt_ref, h, (k + 1) * tq, tq)
    y_in = jnp.concatenate([yp_ref[...], ot_ref[...].T.astype(BF16)], axis=1)
    o_ref[...] = x_ref[...] + gt_ref[...] * _dot(y_in, wo_ref[...])


def _attn(x, mod, yp, qt, kn, kr, vt, wo, *, layer, batch, seq):
    tokens = x.shape[0]
    tq = ATTN_TILE
    nq = seq // tq

    def qrow(width):
        return pl.BlockSpec((tq, width), lambda b, i: (b * nq + i, 0))

    return pl.pallas_call(
        _attn_kernel,
        grid=(batch, nq),
        in_specs=[
            qrow(D_MODEL),
            pl.BlockSpec((None, None, None, 1, D_MODEL), lambda b, i: (layer, 5, b, 0, 0)),
            qrow(POOL_WIDTH),
            pl.BlockSpec((qt.shape[0], tq), lambda b, i: (0, b * nq + i)),
            pl.BlockSpec((seq, N_NOPE), lambda b, i: (b, 0)),
            pl.BlockSpec((seq, LANES), lambda b, i: (b, 0)),
            pl.BlockSpec((N_V, seq), lambda b, i: (0, b)),
            pl.BlockSpec((None, D_MODEL, D_MODEL), lambda b, i: (layer, 0, 0)),
        ],
        out_specs=qrow(D_MODEL),
        out_shape=jax.ShapeDtypeStruct((tokens, D_MODEL), F32),
        scratch_shapes=[pltpu.VMEM((N_V, tq), F32)],
        compiler_params=pltpu.CompilerParams(dimension_semantics=("arbitrary", "arbitrary"),
                                             vmem_limit_bytes=VMEM_LIMIT),
        name="attn_out",
    )(x, mod, yp, qt, kn, kr, vt, wo)


def _pair_cols(w, base, swap):
    cols = []
    for b0 in base:
        first = w[..., b0:b0 + ROPE_HALF]
        second = w[..., b0 + ROPE_HALF:b0 + QK_ROPE]
        cols += [second, first] if swap else [first, second]
    return cols


def _prep_w_in(w_in):
    o_kr = POOL_WIDTH + Q_LORA + KV_LORA
    plain = _pair_cols(w_in, [o_kr, o_kr], False)
    swapped = _pair_cols(w_in, [o_kr, o_kr], True)
    return jnp.concatenate([w_in[..., :o_kr]] + plain + swapped, axis=-1).astype(BF16)


def _prep_w_q_t(w_q_b):
    nope = [w_q_b[..., h * QK_HEAD:h * QK_HEAD + QK_NOPE] for h in range(N_HEADS)]
    bases = [h * QK_HEAD + QK_NOPE for h in range(N_HEADS)]
    w = jnp.concatenate(nope + _pair_cols(w_q_b, bases, False) + _pair_cols(w_q_b, bases, True),
                        axis=-1)
    return jnp.swapaxes(w, -1, -2).astype(BF16)


def _prep_w_kv(w_kv_b):
    width = QK_NOPE + V_HEAD
    k = [w_kv_b[..., h * width:h * width + QK_NOPE] for h in range(N_HEADS)]
    v = [w_kv_b[..., h * width + QK_NOPE:(h + 1) * width] for h in range(N_HEADS)]
    wk = jnp.concatenate(k, axis=-1).astype(BF16)
    wvt = jnp.swapaxes(jnp.concatenate(v, axis=-1), -1, -2).astype(BF16)
    return wk, wvt


def _rope_tables(positions):
    inv_freq = 1.0 / (ROPE_THETA ** (jnp.arange(0, QK_ROPE, 2, dtype=F32) / QK_ROPE))
    ang = positions.astype(F32).reshape(-1, 1) * inv_freq
    cos = jnp.cos(ang)
    sin = jnp.sin(ang)
    cos = jnp.tile(cos, (1, 4))
    sin = jnp.concatenate([-sin, sin, -sin, sin], axis=-1)
    return cos, sin, cos.T, sin.T


def kernel(x, c, positions, ada_w, ada_b, ffn1_norm, ffn1_w_gate, ffn1_w_up, ffn1_w_down, mix_norm, w_in, pool_w, pool_scale, q_a_norm, w_q_b, kv_a_norm, w_kv_b, w_out, ffn2_norm, ffn2_w_gate, ffn2_w_up, ffn2_w_down, final_norm):
    batch, seq, d = x.shape
    n_layers = ada_w.shape[0]
    assert d == D_MODEL and seq % TOKEN_TILE == 0 and seq % ATTN_TILE == 0

    mod = _ada_mod(c, ada_w, ada_b)
    tables = _rope_tables(positions)
    vec = lambda a: a.reshape(n_layers, 1, a.shape[-1])
    bf = lambda a: a.astype(BF16)
    f1 = (ffn1_w_gate, ffn1_w_up, ffn1_w_down)
    f2 = (ffn2_w_gate, ffn2_w_up, ffn2_w_down)
    win, wqt, pw, wo = _prep_w_in(w_in), _prep_w_q_t(w_q_b), bf(pool_w), bf(w_out)
    wk, wvt = _prep_w_kv(w_kv_b)
    n1, nm, n2 = vec(ffn1_norm), vec(mix_norm), vec(ffn2_norm)
    ps, qg, kg = vec(pool_scale), vec(q_a_norm), vec(kv_a_norm)
    fin = final_norm.reshape(1, D_MODEL)

    w_bf = tuple(bf(w[0]) for w in f1)
    xt = x.reshape(batch * seq, D_MODEL)
    for l in range(n_layers):
        last = l == n_layers - 1
        xt, w_bf = _ffn(xt, mod, n1, *w_bf, fin, (*f2, l), layer=l, chunk0=0, seq=seq,
                        final_norm=False)
        yp, qt, kn, kr, vt = _mix_in(xt, mod, nm, win, pw, ps, qg, wqt, kg, wk, wvt, tables,
                                     layer=l, seq=seq)
        xt = _attn(xt, mod, yp, qt, kn, kr, vt, wo, layer=l, batch=batch, seq=seq)
        xt, w_bf = _ffn(xt, mod, n2, *w_bf, fin, None if last else (*f1, l + 1), layer=l, chunk0=6,
                        seq=seq, final_norm=last)
    return xt.reshape(batch, seq, D_MODEL)
```
